```python
import jax, jax.numpy as jnp
from jax import lax
import numpy as np

D_MODEL = 1024
BATCH = 32
SEQ = 2048
DEPTH = 1
DEC_BATCH = 8
DEC_SEQ = 8192
PAST_LEN = 128

N_META = 16
D_MIX = D_MODEL
HG_WIDTH = D_MIX // 2
HG_HEADS = 4
HG_HEAD_DIM = HG_WIDTH // HG_HEADS
CONV_WIDTH = D_MIX - HG_WIDTH
CONV_KERNEL = 31
CONV_PAD = (CONV_KERNEL - 1) // 2
CHUNK = 32
META_PAD = (-N_META) % CHUNK
D_FF = 2816
FFN_RES = 0.5
Z_WIDTH = 5 * HG_WIDTH + 2 * CONV_WIDTH
RMS_EPS = 1e-6
LN_EPS = 1e-5

kernel_name = 'hymba_hgrn2_conformer_encoder'


def rmsnorm(x, g):
    xf = x.astype(jnp.float32)
    y = xf * lax.rsqrt(jnp.mean(xf * xf, axis=-1, keepdims=True) + RMS_EPS)
    return (y * g.astype(jnp.float32)).astype(x.dtype)


def layernorm(x, g, b):
    xf = x.astype(jnp.float32)
    mu = jnp.mean(xf, axis=-1, keepdims=True)
    xc = xf - mu
    y = xc * lax.rsqrt(jnp.mean(xc * xc, axis=-1, keepdims=True) + LN_EPS)
    return (y * g.astype(jnp.float32) + b.astype(jnp.float32)).astype(x.dtype)


def swiglu(x, w_gate, w_up, w_down):
    return (jax.nn.silu(x @ w_gate) * (x @ w_up)) @ w_down


def gla_chunk_scan(q, k, v, logf):
    n, nh, length, dk = q.shape
    dv = v.shape[-1]
    nc = length // CHUNK

    def to_chunks(a):
        return a.astype(jnp.float32).reshape(n, nh, nc, CHUNK, a.shape[-1]).transpose(2, 0, 1, 3, 4)

    qc, kc, vc, gc = to_chunks(q), to_chunks(k), to_chunks(v), to_chunks(logf)
    tri = jnp.tril(jnp.ones((CHUNK, CHUNK), dtype=bool))

    def step(state, blk):
        qb, kb, vb, gb = blk
        b = jnp.cumsum(gb, axis=2)
        b_last = b[:, :, -1:, :]
        o_inter = jnp.einsum('nhtk,nhkv->nhtv', qb * jnp.exp(b), state)
        diff = b[:, :, :, None, :] - b[:, :, None, :, :]
        decay = jnp.exp(jnp.where(tri[:, :, None], diff, -jnp.inf))
        scores = jnp.einsum('nhtk,nhsk,nhtsk->nhts', qb, kb, decay)
        o_intra = jnp.einsum('nhts,nhsv->nhtv', scores, vb)
        new_state = (jnp.exp(b_last[:, :, 0, :, None]) * state
                     + jnp.einsum('nhsk,nhsv->nhkv', kb * jnp.exp(b_last - b), vb))
        return new_state, o_inter + o_intra

    s0 = jnp.zeros((n, nh, dk, dv), jnp.float32)
    _, out = lax.scan(step, s0, (qc, kc, vc, gc))
    return out.transpose(1, 2, 0, 3, 4).reshape(n, nh, length, dv)


def depthwise_conv(u, w, b):
    y = lax.conv_general_dilated(u, w.astype(u.dtype)[:, None, :], window_strides=(1,),
                                 padding=[(CONV_PAD, CONV_PAD)],
                                 dimension_numbers=('NWC', 'WIO', 'NWC'),
                                 feature_group_count=u.shape[-1])
    return y + b.astype(u.dtype)


def token_mixing(u, w_in, lb, hg_norm, conv_w, conv_b, conv_ln_g, conv_ln_b, w_out):
    bsz, t, _ = u.shape
    W, C = HG_WIDTH, CONV_WIDTH
    z = u @ w_in

    def heads(a):
        return a.reshape(bsz, t, HG_HEADS, HG_HEAD_DIM).transpose(0, 2, 1, 3)

    q = jax.nn.silu(heads(z[..., 0:W])).astype(jnp.float32)
    v = heads(z[..., 3 * W:4 * W]).astype(jnp.float32)
    f_logits = jnp.stack([heads(z[..., W:2 * W]), heads(z[..., 2 * W:3 * W])]).astype(jnp.float32)
    lb = lb.reshape(2, 1, HG_HEADS, 1, HG_HEAD_DIM)
    logf = jnp.logaddexp(jnp.log(lb), jnp.log1p(-lb) + jax.nn.log_sigmoid(f_logits))
    k = -jnp.expm1(logf)

    def pad(a):
        return jnp.pad(a, ((0, 0),) * (a.ndim - 2) + ((META_PAD, 0), (0, 0)))

    def flip(a):
        return jnp.flip(a, axis=-2)

    qp, vp, kp, gp = pad(q), pad(v), pad(k), pad(logf)
    qs = jnp.concatenate([qp, flip(qp)], axis=0)
    vs = jnp.concatenate([vp, flip(vp)], axis=0)
    ks = jnp.concatenate([kp[0], flip(kp[1])], axis=0)
    gs = jnp.concatenate([gp[0], flip(gp[1])], axis=0)
    o = gla_chunk_scan(qs, ks, vs, gs)
    o = (o[:bsz] + flip(o[bsz:]))[:, :, META_PAD:]
    o = o * lax.rsqrt(jnp.mean(o * o, axis=-1, keepdims=True) + RMS_EPS)
    o = o.transpose(0, 2, 1, 3).reshape(bsz, t, W) * hg_norm.astype(jnp.float32)
    hg_out = (o * jax.nn.silu(z[..., 4 * W:5 * W].astype(jnp.float32))).astype(u.dtype)

    glu = z[..., 5 * W:5 * W + C] * jax.nn.sigmoid(z[..., 5 * W + C:5 * W + 2 * C])
    c = depthwise_conv(glu, conv_w, conv_b)
    c = jax.nn.silu(layernorm(c, conv_ln_g, conv_ln_b))

    return jnp.concatenate([hg_out, c], axis=-1) @ w_out


def encoder_trunk(x, meta_tokens, ffn1_norm, ffn1_w_gate, ffn1_w_up, ffn1_w_down, mix_norm, w_in,
                  lb_logits, hg_norm, conv_w, conv_b, conv_ln_g, conv_ln_b, w_out,
                  ffn2_norm, ffn2_w_gate, ffn2_w_up, ffn2_w_down, final_norm):
    bsz = x.shape[0]
    meta = jnp.broadcast_to(meta_tokens[None].astype(x.dtype), (bsz, N_META, x.shape[-1]))
    h = jnp.concatenate([meta, x], axis=1)
    lb_all = jnp.cumsum(jax.nn.softmax(lb_logits.astype(jnp.float32), axis=0), axis=0)
    for layer in range(DEPTH):
        h = h + FFN_RES * swiglu(rmsnorm(h, ffn1_norm[layer]), ffn1_w_gate[layer], ffn1_w_up[layer], ffn1_w_down[layer])
        h = h + token_mixing(rmsnorm(h, mix_norm[layer]), w_in[layer], lb_all[layer], hg_norm[layer],
                             conv_w[layer], conv_b[layer], conv_ln_g[layer], conv_ln_b[layer], w_out[layer])
        h = h + FFN_RES * swiglu(rmsnorm(h, ffn2_norm[layer]), ffn2_w_gate[layer], ffn2_w_up[layer], ffn2_w_down[layer])
    return rmsnorm(h, final_norm)[:, N_META:]


def setup_inputs(seed: int = 0) -> dict:
    key = jax.random.key(seed)
    ks = jax.random.split(key, 24)
    f32 = jnp.float32

    def nrm(k, shape, scale):
        return jax.random.normal(k, shape, f32) * scale

    def gain(k, shape):
        return 1.0 + 0.02 * jax.random.normal(k, shape, f32)

    return {
        'x_prompt': nrm(ks[0], (BATCH, SEQ, D_MODEL), 1.0),
        'x_sample': nrm(ks[1], (DEC_BATCH, DEC_SEQ, D_MODEL), 1.0),
        'meta_tokens': nrm(ks[2], (N_META, D_MODEL), 1.0),
        'ffn1_norm': gain(ks[3], (DEPTH, D_MODEL)),
        'ffn1_w_gate': nrm(ks[4], (DEPTH, D_MODEL, D_FF), D_MODEL ** -0.5),
        'ffn1_w_up': nrm(ks[5], (DEPTH, D_MODEL, D_FF), D_MODEL ** -0.5),
        'ffn1_w_down': nrm(ks[6], (DEPTH, D_FF, D_MODEL), D_FF ** -0.5),
        'mix_norm': gain(ks[7], (DEPTH, D_MODEL)),
        'w_in': nrm(ks[8], (DEPTH, D_MODEL, Z_WIDTH), D_MODEL ** -0.5),
        'lb_logits': nrm(ks[9], (DEPTH + 1, 2, HG_WIDTH), 0.5),
        'hg_norm': gain(ks[10], (DEPTH, HG_WIDTH)),
        'conv_w': nrm(ks[11], (DEPTH, CONV_KERNEL, CONV_WIDTH), CONV_KERNEL ** -0.5),
        'conv_b': nrm(ks[12], (DEPTH, CONV_WIDTH), 0.02),
        'conv_ln_g': gain(ks[13], (DEPTH, CONV_WIDTH)),
        'conv_ln_b': nrm(ks[14], (DEPTH, CONV_WIDTH), 0.02),
        'w_out': nrm(ks[15], (DEPTH, D_MIX, D_MODEL), D_MIX ** -0.5),
        'ffn2_norm': gain(ks[16], (DEPTH, D_MODEL)),
        'ffn2_w_gate': nrm(ks[17], (DEPTH, D_MODEL, D_FF), D_MODEL ** -0.5),
        'ffn2_w_up': nrm(ks[18], (DEPTH, D_MODEL, D_FF), D_MODEL ** -0.5),
        'ffn2_w_down': nrm(ks[19], (DEPTH, D_FF, D_MODEL), D_FF ** -0.5),
        'final_norm': gain(ks[20], (D_MODEL,)),
    }


def reference(x_prompt, x_sample, meta_tokens, ffn1_norm, ffn1_w_gate, ffn1_w_up, ffn1_w_down,
              mix_norm, w_in, lb_logits, hg_norm, conv_w, conv_b, conv_ln_g, conv_ln_b, w_out,
              ffn2_norm, ffn2_w_gate, ffn2_w_up, ffn2_w_down, final_norm):
    y_prompt = encoder_trunk(x_prompt, meta_tokens, ffn1_norm, ffn1_w_gate, ffn1_w_up, ffn1_w_down,
                             mix_norm, w_in, lb_logits, hg_norm, conv_w, conv_b, conv_ln_g, conv_ln_b,
                             w_out, ffn2_norm, ffn2_w_gate, ffn2_w_up, ffn2_w_down, final_norm)
    y_sample = encoder_trunk(x_sample, meta_tokens, ffn1_norm, ffn1_w_gate, ffn1_w_up, ffn1_w_down,
                             mix_norm, w_in, lb_logits, hg_norm, conv_w, conv_b, conv_ln_g, conv_ln_b,
                             w_out, ffn2_norm, ffn2_w_gate, ffn2_w_up, ffn2_w_down, final_norm)
    return (y_prompt, y_sample)
```

```python
import functools

import jax
import jax.numpy as jnp
from jax import lax
from jax.experimental import pallas as pl
from jax.experimental.pallas import tpu as pltpu

D_MODEL = 1024
D_FF = 2816
N_META = 16
HG_WIDTH = 512
HG_HEADS = 4
HEAD_DIM = 128
CONV_WIDTH = 512
CONV_KERNEL = 31
CONV_PAD = 15
Z_WIDTH = 5 * HG_WIDTH + 2 * CONV_WIDTH
RMS_EPS = 1e-6
LN_EPS = 1e-5
FFN_RES = 0.5

SUBLANES = 8
CHUNK = 32
HALO = 16
FF_CHUNK = 256
TOKEN_TILE = 512
SEQ_TILE = 512
SAFE_LOG_DECAY = 60.0
VMEM_LIMIT_BYTES = 56 * 1024 * 1024

BF16 = jnp.bfloat16
F32 = jnp.float32


def _rmsnorm(x, g):
    return x * lax.rsqrt(jnp.mean(x * x, axis=-1, keepdims=True) + RMS_EPS) * g


def _sigmoid(x):
    return 1.0 / (1.0 + jnp.exp(-x))


def _dot(a, b):
    return jnp.dot(a, b, preferred_element_type=F32)


def _dot_nt(a, b):
    return lax.dot_general(a, b, (((1,), (1,)), ((), ())), preferred_element_type=F32)


def _dot_tn(a, b):
    return lax.dot_general(a, b, (((0,), (0,)), ((), ())), preferred_element_type=F32)


def _const_spec(shape):
    zeros = (0,) * len(shape)
    return pl.BlockSpec(shape, lambda *_: zeros, pipeline_mode=pl.Buffered(1))


def _swiglu_residual(x, g_ref, wg_ref, wu_ref, wd_ref):
    xb = _rmsnorm(x, g_ref[...]).astype(BF16)
    acc = jnp.zeros(x.shape, F32)
    for c in range(wg_ref.shape[0]):
        hg = _dot(xb, wg_ref[c])
        hu = _dot(xb, wu_ref[c])
        hid = (hg * _sigmoid(hg) * hu).astype(BF16)
        acc = acc + _dot(hid, wd_ref[c])
    return x + FFN_RES * acc


def _ffn1_body(x_ref, g_ref, wg_ref, wu_ref, wd_ref, o_ref):
    o_ref[...] = _swiglu_residual(x_ref[...], g_ref, wg_ref, wu_ref, wd_ref)


def _ffn2_body(h_ref, mix_ref, wo_ref, g_ref, wg_ref, wu_ref, wd_ref, fn_ref, o_ref):
    h2 = h_ref[...] + _dot(mix_ref[...].astype(BF16), wo_ref[...])
    h3 = _swiglu_residual(h2, g_ref, wg_ref, wu_ref, wd_ref)
    o_ref[...] = _rmsnorm(h3, fn_ref[...])


def _inproj_body(h_ref, g_ref, w_ref, z_ref):
    u = _rmsnorm(h_ref[...], g_ref[...]).astype(BF16)
    z_ref[...] = _dot(u, w_ref[...])


def _ffn_weight_specs():
    nc = D_FF // FF_CHUNK
    return [_const_spec((1, D_MODEL)),
            _const_spec((nc, D_MODEL, FF_CHUNK)),
            _const_spec((nc, D_MODEL, FF_CHUNK)),
            _const_spec((nc, FF_CHUNK, D_MODEL))]


def _token_params():
    return pltpu.CompilerParams(dimension_semantics=("parallel",), vmem_limit_bytes=VMEM_LIMIT_BYTES)


def _ffn1(x, w, tm):
    n = x.shape[0]
    row = pl.BlockSpec((tm, D_MODEL), lambda i: (i, 0))
    return pl.pallas_call(
        _ffn1_body, grid=(n // tm,),
        in_specs=[row] + _ffn_weight_specs(), out_specs=row,
        out_shape=jax.ShapeDtypeStruct((n, D_MODEL), F32),
        compiler_params=_token_params(), name="ffn1",
    )(x, w["g1"], w["wg1"], w["wu1"], w["wd1"])


def _ffn2(h1, mix, w, tm):
    n = h1.shape[0]
    row = pl.BlockSpec((tm, D_MODEL), lambda i: (i, 0))
    return pl.pallas_call(
        _ffn2_body, grid=(n // tm,),
        in_specs=[row, row, _const_spec((D_MODEL, D_MODEL))] + _ffn_weight_specs() + [_const_spec((1, D_MODEL))],
        out_specs=row,
        out_shape=jax.ShapeDtypeStruct((n, D_MODEL), F32),
        compiler_params=_token_params(), name="ffn2",
    )(h1, mix, w["wo"], w["g2"], w["wg2"], w["wu2"], w["wd2"], w["gf"])


def _inproj(h1, w, tm):
    n = h1.shape[0]
    return pl.pallas_call(
        _inproj_body, grid=(n // tm,),
        in_specs=[pl.BlockSpec((tm, D_MODEL), lambda i: (i, 0)),
                  _const_spec((1, D_MODEL)), _const_spec((D_MODEL, Z_WIDTH))],
        out_specs=pl.BlockSpec((tm, Z_WIDTH), lambda i: (i, 0)),
        out_shape=jax.ShapeDtypeStruct((n, Z_WIDTH), F32),
        compiler_params=_token_params(), name="inproj",
    )(h1, w["gm"], w["win"])


def _decay_terms(xf, lbl_ref, direction):
    a0 = lbl_ref[0, direction:direction + 1, :]
    a1 = lbl_ref[1, direction:direction + 1, :]
    m = jnp.maximum(a0, a1)
    lse = m + jnp.log(jnp.exp(a0 - m) + jnp.exp(a1 - m))
    log_lb = a0 - lse
    log_1mlb = a1 - lse
    t = jnp.exp(-jnp.abs(xf))
    inv = 1.0 / (1.0 + t)
    log_sig = jnp.minimum(xf, 0.0) - jnp.log(1.0 + t)
    u = log_1mlb + log_sig
    logf = jnp.maximum(log_lb, u) + jnp.log(1.0 + jnp.exp(-jnp.abs(log_lb - u)))
    k = jnp.exp(log_1mlb) * jnp.where(xf >= 0.0, t, 1.0) * inv
    return logf, k


def _chunk_cumsum(x, chunk):
    rows = x.shape[0]
    pos = lax.broadcasted_iota(jnp.int32, x.shape, 0) % chunk
    d = 1
    while d < chunk:
        shifted = pltpu.roll(x, d, 0)
        x = x + jnp.where(pos >= d, shifted, 0.0)
        d *= 2
    del rows
    return x


def _chunk_row_broadcast(x, chunk, row):
    rows, lanes = x.shape
    x3 = x.reshape(rows // chunk, chunk, lanes)
    return jnp.broadcast_to(x3[:, row:row + 1, :], x3.shape).reshape(rows, lanes)


def _recurrence_prepare(q_raw, xf, lbl_ref, direction, chunk, q_s, k_s, b_s, qd_s, kd_s, kp_s):
    reverse = direction == 1
    q = q_raw * _sigmoid(q_raw)
    logf, k = _decay_terms(xf, lbl_ref, direction)
    pre = _chunk_cumsum(logf, chunk)
    tot = _chunk_row_broadcast(pre, chunk, chunk - 1)
    b = tot - pre + logf if reverse else pre
    q_s[...] = q
    k_s[...] = k
    b_s[...] = b
    qd_s[...] = (q * jnp.exp(b)).astype(BF16)
    kd_s[...] = (k * jnp.exp(jnp.minimum(-b, 2.0 * SAFE_LOG_DECAY))).astype(BF16)
    kp_s[...] = (k * jnp.exp(tot - b)).astype(BF16)
    return jnp.min(tot)


def _chunk_step(r0, chunk, reverse, exact, v_ref, st_ref, o_s, q_s, k_s, b_s, qd_s, kd_s, kp_s):
    rows = pl.ds(r0, chunk)
    exit_rows = pl.ds(r0 if reverse else r0 + chunk - SUBLANES, SUBLANES)
    exit_sub = 0 if reverse else SUBLANES - 1
    ti = lax.broadcasted_iota(jnp.int32, (chunk, chunk), 0)
    si = lax.broadcasted_iota(jnp.int32, (chunk, chunk), 1)
    causal = (ti <= si) if reverse else (ti >= si)
    for h in range(HG_HEADS):
        hs = slice(h * HEAD_DIM, (h + 1) * HEAD_DIM)
        st = st_ref[h]
        kp = kp_s[rows, hs]
        vf = v_ref[rows, hs]
        vb = vf.astype(BF16)
        qd = qd_s[rows, hs]
        o = _dot_nt(qd, st.astype(BF16))
        if exact:
            qf = q_s[rows, hs]
            kf = k_s[rows, hs]
            bf = b_s[rows, hs]
            trow = lax.broadcasted_iota(jnp.int32, (chunk, 1), 0)

            def s_body(s, acc, qf=qf, kf=kf, bf=bf, vf=vf, trow=trow):
                def row(a):
                    return jnp.sum(jnp.where(trow == s, a, 0.0), axis=0, keepdims=True)
                w = jnp.exp(jnp.minimum(bf - row(bf), 0.0))
                col = jnp.sum(qf * row(kf) * w, axis=-1, keepdims=True)
                keep = (trow <= s) if reverse else (trow >= s)
                return acc + jnp.where(keep, col, 0.0) * row(vf)

            o = o + lax.fori_loop(0, chunk, s_body, jnp.zeros((chunk, HEAD_DIM), F32))
        else:
            sc = _dot_nt(qd, kd_s[rows, hs])
            p = jnp.where(causal, sc, 0.0).astype(BF16)
            o = o + _dot(p, vb)
        o_s[rows, hs] = o
        dec = jnp.exp(b_s[exit_rows, hs][exit_sub:exit_sub + 1, :])
        st_ref[h] = st * dec + _dot_tn(vb, kp)


def _run_recurrence(q_raw, xf, v_ref, lbl_ref, direction, st_ref, o_s, scr):
    q_s, k_s, b_s, qd_s, kd_s, kp_s = scr
    reverse = direction == 1
    n_rows = q_raw.shape[0]
    n_chunks = n_rows // CHUNK
    tot_min = _recurrence_prepare(q_raw, xf, lbl_ref, direction, CHUNK, q_s, k_s, b_s, qd_s, kd_s, kp_s)

    def loop(exact):
        def body(i, carry):
            c = (n_chunks - 1 - i) if reverse else i
            r0 = pl.multiple_of(c * CHUNK, CHUNK)
            _chunk_step(r0, CHUNK, reverse, exact, v_ref, st_ref, o_s, q_s, k_s, b_s, qd_s, kd_s, kp_s)
            return carry
        lax.fori_loop(0, n_chunks, body, 0)

    safe = tot_min >= -SAFE_LOG_DECAY

    @pl.when(safe)
    def _():
        loop(False)

    @pl.when(jnp.logical_not(safe))
    def _():
        loop(True)


def _gla_scratch(ts):
    return [pltpu.VMEM((HG_HEADS, HEAD_DIM, HEAD_DIM), F32),
            pltpu.VMEM((ts, HG_WIDTH), F32),
            pltpu.VMEM((ts, HG_WIDTH), F32),
            pltpu.VMEM((ts, HG_WIDTH), F32),
            pltpu.VMEM((ts, HG_WIDTH), F32),
            pltpu.VMEM((ts, HG_WIDTH), BF16),
            pltpu.VMEM((ts, HG_WIDTH), BF16),
            pltpu.VMEM((ts, HG_WIDTH), BF16)]


def _gla_bwd_body(q_ref, f_ref, v_ref, lbl_ref, o_ref, st_ref, o_s, *scr):
    @pl.when(pl.program_id(1) == 0)
    def _():
        st_ref[...] = jnp.zeros(st_ref.shape, F32)

    _run_recurrence(q_ref[...], f_ref[...], v_ref, lbl_ref, 1, st_ref, o_s, scr)
    o_ref[...] = o_s[...]


def _gla_bwd(z, lbl, ts):
    bsz, seq, _ = z.shape
    nt = seq // ts

    def col(j):
        return pl.BlockSpec((None, ts, HG_WIDTH), lambda b, i, j=j: (b, nt - 1 - i, j))

    return pl.pallas_call(
        _gla_bwd_body, grid=(bsz, nt),
        in_specs=[col(0), col(2), col(3), _const_spec((2, 2, HG_WIDTH))],
        out_specs=pl.BlockSpec((None, ts, HG_WIDTH), lambda b, i: (b, nt - 1 - i, 0)),
        out_shape=jax.ShapeDtypeStruct((bsz, seq, HG_WIDTH), F32),
        scratch_shapes=_gla_scratch(ts),
        compiler_params=pltpu.CompilerParams(dimension_semantics=("parallel", "arbitrary"),
                                             vmem_limit_bytes=VMEM_LIMIT_BYTES),
        name="gla_bwd",
    )(z, z, z, lbl)


def _meta_state(zm_ref, lbl_ref, st_ref):
    xf = zm_ref[:, HG_WIDTH:2 * HG_WIDTH]
    logf, k = _decay_terms(xf, lbl_ref, 0)
    pre = _chunk_cumsum(logf, N_META)
    tot = jnp.broadcast_to(pre[N_META - 1:N_META, :], pre.shape)
    kp = (k * jnp.exp(tot - pre)).astype(BF16)
    vb = zm_ref[:, 3 * HG_WIDTH:4 * HG_WIDTH].astype(BF16)
    for h in range(HG_HEADS):
        hs = slice(h * HEAD_DIM, (h + 1) * HEAD_DIM)
        st_ref[h] = _dot_tn(vb[:, hs], kp[:, hs])


def _mix_body(q_ref, f_ref, v_ref, gate_ref, a_ref, g_ref, ap_ref, gp_ref, an_ref, gn_ref, ob_ref, zm_ref,
              lbl_ref, hgn_ref, cw_ref, cb_ref, lng_ref, lnb_ref, o_ref, st_ref, o_s, e_s, *scr):
    i = pl.program_id(1)
    nt = pl.num_programs(1)
    ts = q_ref.shape[0]

    @pl.when(i == 0)
    def _():
        _meta_state(zm_ref, lbl_ref, st_ref)

    _run_recurrence(q_ref[...], f_ref[...], v_ref, lbl_ref, 0, st_ref, o_s, scr)

    gate = gate_ref[...]
    gate = gate * _sigmoid(gate)
    for h in range(HG_HEADS):
        hs = slice(h * HEAD_DIM, (h + 1) * HEAD_DIM)
        o = o_s[:, hs] + ob_ref[:, hs]
        o = o * lax.rsqrt(jnp.mean(o * o, axis=-1, keepdims=True) + RMS_EPS)
        o_ref[:, hs] = o * hgn_ref[:, hs] * gate[:, hs]

    def glu(a, g):
        return a * _sigmoid(g)

    meta_glu = glu(zm_ref[:, 5 * HG_WIDTH:5 * HG_WIDTH + CONV_WIDTH], zm_ref[:, 5 * HG_WIDTH + CONV_WIDTH:])
    prev = jnp.where(i == 0, meta_glu, glu(ap_ref[...], gp_ref[...]))
    nxt = jnp.where(i == nt - 1, 0.0, glu(an_ref[...], gn_ref[...]))
    e_s[0:HALO, :] = prev
    e_s[HALO:HALO + ts, :] = glu(a_ref[...], g_ref[...])
    e_s[HALO + ts:HALO + ts + HALO, :] = nxt
    acc = jnp.zeros((ts, CONV_WIDTH), F32) + cb_ref[...]
    for j in range(CONV_KERNEL):
        off = HALO - CONV_PAD + j
        acc = acc + e_s[off:off + ts, :] * cw_ref[j:j + 1, :]
    mu = jnp.mean(acc, axis=-1, keepdims=True)
    xc = acc - mu
    y = xc * lax.rsqrt(jnp.mean(xc * xc, axis=-1, keepdims=True) + LN_EPS) * lng_ref[...] + lnb_ref[...]
    o_ref[:, HG_WIDTH:] = y * _sigmoid(y)


def _mix(z, ob, zm, w, ts):
    bsz, seq, _ = z.shape
    nt = seq // ts
    hb = ts // HALO
    nhb = seq // HALO

    def col(j):
        return pl.BlockSpec((None, ts, HG_WIDTH), lambda b, i, j=j: (b, i, j))

    def prev_halo(j):
        return pl.BlockSpec((None, HALO, HG_WIDTH), lambda b, i, j=j: (b, jnp.maximum(i * hb - 1, 0), j))

    def next_halo(j):
        return pl.BlockSpec((None, HALO, HG_WIDTH), lambda b, i, j=j: (b, jnp.minimum((i + 1) * hb, nhb - 1), j))

    row = lambda width: _const_spec((1, width))
    return pl.pallas_call(
        _mix_body, grid=(bsz, nt),
        in_specs=[col(0), col(1), col(3), col(4), col(5), col(6),
                  prev_halo(5), prev_halo(6), next_halo(5), next_halo(6),
                  pl.BlockSpec((None, ts, HG_WIDTH), lambda b, i: (b, i, 0)),
                  _const_spec((N_META, Z_WIDTH)), _const_spec((2, 2, HG_WIDTH)),
                  row(HG_WIDTH), _const_spec((CONV_KERNEL, CONV_WIDTH)), row(CONV_WIDTH),
                  row(CONV_WIDTH), row(CONV_WIDTH)],
        out_specs=pl.BlockSpec((None, ts, D_MODEL), lambda b, i: (b, i, 0)),
        out_shape=jax.ShapeDtypeStruct((bsz, seq, D_MODEL), F32),
        scratch_shapes=(_gla_scratch(ts)[:2] + [pltpu.VMEM((ts + 2 * HALO, CONV_WIDTH), F32)]
                        + _gla_scratch(ts)[2:]),
        compiler_params=pltpu.CompilerParams(dimension_semantics=("parallel", "arbitrary"),
                                             vmem_limit_bytes=VMEM_LIMIT_BYTES),
        name="mix",
    )(z, z, z, z, z, z, z, z, z, z, ob, zm, w["lbl"], w["hgn"], w["cw"], w["cb"], w["lng"], w["lnb"])


def _prep_weights(ffn1_norm, ffn1_w_gate, ffn1_w_up, ffn1_w_down, mix_norm, w_in, lb_logits, hg_norm, conv_w,
                  conv_b, conv_ln_g, conv_ln_b, w_out, ffn2_norm, ffn2_w_gate, ffn2_w_up, ffn2_w_down, final_norm):
    nc = D_FF // FF_CHUNK

    def cols(wm):
        return wm.astype(BF16).reshape(D_MODEL, nc, FF_CHUNK).transpose(1, 0, 2)

    def rows(wm):
        return wm.astype(BF16).reshape(nc, FF_CHUNK, D_MODEL)

    return dict(
        g1=ffn1_norm[0][None], wg1=cols(ffn1_w_gate[0]), wu1=cols(ffn1_w_up[0]), wd1=rows(ffn1_w_down[0]),
        gm=mix_norm[0][None], win=w_in[0].astype(BF16), lbl=lb_logits.astype(F32), hgn=hg_norm[0][None],
        cw=conv_w[0], cb=conv_b[0][None], lng=conv_ln_g[0][None], lnb=conv_ln_b[0][None],
        wo=w_out[0].astype(BF16),
        g2=ffn2_norm[0][None], wg2=cols(ffn2_w_gate[0]), wu2=cols(ffn2_w_up[0]), wd2=rows(ffn2_w_down[0]),
        gf=final_norm[None])


def _trunk(x, zm, w, tm, ts):
    bsz, seq, _ = x.shape
    xf = x.reshape(bsz * seq, D_MODEL)
    h1 = _ffn1(xf, w, tm)
    z = _inproj(h1, w, tm).reshape(bsz, seq, Z_WIDTH)
    ob = _gla_bwd(z, w["lbl"], ts)
    mix = _mix(z, ob, zm, w, ts).reshape(bsz * seq, D_MODEL)
    return _ffn2(h1, mix, w, tm).reshape(bsz, seq, D_MODEL)


def kernel(x_prompt, x_sample, meta_tokens, ffn1_norm, ffn1_w_gate, ffn1_w_up, ffn1_w_down, mix_norm, w_in,
           lb_logits, hg_norm, conv_w, conv_b, conv_ln_g, conv_ln_b, w_out, ffn2_norm, ffn2_w_gate, ffn2_w_up,
           ffn2_w_down, final_norm):
    tm, ts = TOKEN_TILE, SEQ_TILE
    w = _prep_weights(ffn1_norm, ffn1_w_gate, ffn1_w_up, ffn1_w_down, mix_norm, w_in, lb_logits, hg_norm,
                      conv_w, conv_b, conv_ln_g, conv_ln_b, w_out, ffn2_norm, ffn2_w_gate, ffn2_w_up,
                      ffn2_w_down, final_norm)
    zm = _inproj(_ffn1(meta_tokens.astype(F32), w, N_META), w, N_META)
    return (_trunk(x_prompt, zm, w, tm, ts), _trunk(x_sample, zm, w, tm, ts))
```

```python
import jax
import jax.numpy as jnp
from jax import lax
from jax.experimental import pallas as pl
from jax.experimental.pallas import tpu as pltpu

D_MODEL = 1024
D_FF = 2816
N_META = 16
HG_WIDTH = 512
HG_HEADS = 4
HEAD_DIM = 128
CONV_WIDTH = 512
CONV_KERNEL = 31
CONV_PAD = 15
Z_WIDTH = 5 * HG_WIDTH + 2 * CONV_WIDTH
RMS_EPS = 1e-6
LN_EPS = 1e-5
FFN_RES = 0.5

SUBLANES = 8
LANES = 128
CONV_ROWS = 128
CHUNK = 32
HALO = 16
FF_CHUNK = 256
TOKEN_TILE = 512
SEQ_TILE = 512
SAFE_LOG_DECAY = 60.0
MAX_SAFE_EXP = 80.0
VMEM_LIMIT_BYTES = 56 * 1024 * 1024

BF16 = jnp.bfloat16
F32 = jnp.float32


def _rmsnorm(x, g):
    return x * lax.rsqrt(jnp.mean(x * x, axis=-1, keepdims=True) + RMS_EPS) * g


def _sigmoid(x):
    return 1.0 / (1.0 + jnp.exp(-x))


def _dot(a, b):
    return jnp.dot(a, b, preferred_element_type=F32)


def _dot_nt(a, b):
    return lax.dot_general(a, b, (((1,), (1,)), ((), ())), preferred_element_type=F32)


def _dot_tn(a, b):
    return lax.dot_general(a, b, (((0,), (0,)), ((), ())), preferred_element_type=F32)


def _const_spec(shape):
    zeros = (0,) * len(shape)
    return pl.BlockSpec(shape, lambda *_: zeros, pipeline_mode=pl.Buffered(1))


def _swiglu_residual(x, g_ref, wg_ref, wu_ref, wd_ref):
    xb = _rmsnorm(x, g_ref[...]).astype(BF16)
    acc = jnp.zeros(x.shape, F32)
    for c in range(wg_ref.shape[0]):
        hg = _dot(xb, wg_ref[c])
        hu = _dot(xb, wu_ref[c])
        hid = (hg * _sigmoid(hg) * hu).astype(BF16)
        acc = acc + _dot(hid, wd_ref[c])
    return x + FFN_RES * acc


def _ffn1_body(x_ref, g_ref, wg_ref, wu_ref, wd_ref, o_ref):
    o_ref[...] = _swiglu_residual(x_ref[...], g_ref, wg_ref, wu_ref, wd_ref)


def _ffn2_body(h_ref, mix_ref, wo_ref, g_ref, wg_ref, wu_ref, wd_ref, fn_ref, o_ref):
    h2 = h_ref[...] + _dot(mix_ref[...].astype(BF16), wo_ref[...])
    h3 = _swiglu_residual(h2, g_ref, wg_ref, wu_ref, wd_ref)
    o_ref[...] = _rmsnorm(h3, fn_ref[...])


def _inproj_body(h_ref, g_ref, w_ref, z_ref):
    u = _rmsnorm(h_ref[...], g_ref[...]).astype(BF16)
    z_ref[...] = _dot(u, w_ref[...])


def _ffn_weight_specs():
    nc = D_FF // FF_CHUNK
    return [_const_spec((1, D_MODEL)),
            _const_spec((nc, D_MODEL, FF_CHUNK)),
            _const_spec((nc, D_MODEL, FF_CHUNK)),
            _const_spec((nc, FF_CHUNK, D_MODEL))]


def _token_params():
    return pltpu.CompilerParams(dimension_semantics=("parallel",), vmem_limit_bytes=VMEM_LIMIT_BYTES)


def _ffn1(x, w, tm):
    n = x.shape[0]
    row = pl.BlockSpec((tm, D_MODEL), lambda i: (i, 0))
    return pl.pallas_call(
        _ffn1_body, grid=(n // tm,),
        in_specs=[row] + _ffn_weight_specs(), out_specs=row,
        out_shape=jax.ShapeDtypeStruct((n, D_MODEL), F32),
        compiler_params=_token_params(), name="ffn1",
    )(x, w["g1"], w["wg1"], w["wu1"], w["wd1"])


def _ffn2(h1, mix, w, tm):
    n = h1.shape[0]
    row = pl.BlockSpec((tm, D_MODEL), lambda i: (i, 0))
    return pl.pallas_call(
        _ffn2_body, grid=(n // tm,),
        in_specs=[row, row, _const_spec((D_MODEL, D_MODEL))] + _ffn_weight_specs() + [_const_spec((1, D_MODEL))],
        out_specs=row,
        out_shape=jax.ShapeDtypeStruct((n, D_MODEL), F32),
        compiler_params=_token_params(), name="ffn2",
    )(h1, mix, w["wo"], w["g2"], w["wg2"], w["wu2"], w["wd2"], w["gf"])


def _inproj(h1, w, tm):
    n = h1.shape[0]
    return pl.pallas_call(
        _inproj_body, grid=(n // tm,),
        in_specs=[pl.BlockSpec((tm, D_MODEL), lambda i: (i, 0)),
                  _const_spec((1, D_MODEL)), _const_spec((D_MODEL, Z_WIDTH))],
        out_specs=pl.BlockSpec((tm, Z_WIDTH), lambda i: (i, 0)),
        out_shape=jax.ShapeDtypeStruct((n, Z_WIDTH), F32),
        compiler_params=_token_params(), name="inproj",
    )(h1, w["gm"], w["win"])


def _decay_consts(lbl_ref, direction):
    a0 = lbl_ref[0, direction:direction + 1, :]
    a1 = lbl_ref[1, direction:direction + 1, :]
    m = jnp.maximum(a0, a1)
    lse = m + jnp.log(jnp.exp(a0 - m) + jnp.exp(a1 - m))
    return a0 - lse, a1 - lse


def _decay_terms(xf, consts):
    log_lb, log_1mlb = consts
    t = jnp.exp(-jnp.abs(xf))
    inv = 1.0 / (1.0 + t)
    log_sig = jnp.minimum(xf, 0.0) - jnp.log(1.0 + t)
    u = log_1mlb + log_sig
    logf = jnp.maximum(log_lb, u) + jnp.log(1.0 + jnp.exp(-jnp.abs(log_lb - u)))
    k = jnp.exp(log_1mlb) * jnp.where(xf >= 0.0, t, 1.0) * inv
    return logf, k


def _cumsum_rows(x):
    pos = lax.broadcasted_iota(jnp.int32, x.shape, 0)
    d = 1
    while d < x.shape[0]:
        x = x + jnp.where(pos >= d, pltpu.roll(x, d, 0), 0.0)
        d *= 2
    return x


def _chunk_prepare(q_raw, xf, consts, reverse):
    q = q_raw * _sigmoid(q_raw)
    logf, k = _decay_terms(xf, consts)
    pre = _cumsum_rows(logf)
    tot = pre[pre.shape[0] - 1:, :]
    b = tot - pre + logf if reverse else pre
    return q, k, b, tot


def _exact_intra(q, k, b, v, reverse):
    chunk = q.shape[0]
    trow = lax.broadcasted_iota(jnp.int32, (chunk, 1), 0)

    def s_body(s, acc):
        def row(a):
            return jnp.sum(jnp.where(trow == s, a, 0.0), axis=0, keepdims=True)
        w = jnp.exp(jnp.minimum(b - row(b), 0.0))
        col = jnp.sum(q * row(k) * w, axis=-1, keepdims=True)
        keep = (trow <= s) if reverse else (trow >= s)
        return acc + jnp.where(keep, col, 0.0) * row(v)

    return lax.fori_loop(0, chunk, s_body, jnp.zeros(q.shape, F32))


def _head(h):
    return slice(h * HEAD_DIM, (h + 1) * HEAD_DIM)


def _exact_chunk(q, k, b, tot, v, st_ref, reverse):
    qd = (q * jnp.exp(b)).astype(BF16)
    kp = (k * jnp.exp(tot - b)).astype(BF16)
    dec = jnp.exp(tot)
    outs = []
    for h in range(HG_HEADS):
        hs = _head(h)
        st = st_ref[h]
        outs.append(_dot_nt(qd[:, hs], st.astype(BF16))
                    + _exact_intra(q[:, hs], k[:, hs], b[:, hs], v[:, hs], reverse))
        st_ref[h] = st * dec[:, hs] + _dot_tn(v[:, hs].astype(BF16), kp[:, hs])
    return jnp.concatenate(outs, axis=-1)


def _factorised_tile(q_ref, f_ref, v_ref, consts, reverse, st_ref, o_s):
    n_chunks = q_ref.shape[0] // CHUNK
    order = list(range(n_chunks - 1, -1, -1) if reverse else range(n_chunks))
    ti = lax.broadcasted_iota(jnp.int32, (CHUNK, CHUNK), 0)
    si = lax.broadcasted_iota(jnp.int32, (CHUNK, CHUNK), 1)
    causal = (ti <= si) if reverse else (ti >= si)
    prep, incr, both = {}, {}, {}

    def rows(c):
        return slice(c * CHUNK, (c + 1) * CHUNK)

    def prepare(c):
        q, k, b, tot = _chunk_prepare(q_ref[rows(c), :], f_ref[rows(c), :], consts, reverse)
        vb = v_ref[rows(c), :].astype(BF16)
        kp = (k * jnp.exp(tot - b)).astype(BF16)
        prep[c] = dict(qd=(q * jnp.exp(b)).astype(BF16),
                       kd=(k * jnp.exp(jnp.minimum(-b, MAX_SAFE_EXP))).astype(BF16),
                       vb=vb, dec=jnp.exp(tot), tot=tot)
        incr[c] = [_dot_tn(vb[:, _head(h)], kp[:, _head(h)]) for h in range(HG_HEADS)]

    def carry_and_scores(c):
        p = prep[c]
        both[c] = [_dot_nt(p["qd"][:, _head(h)],
                           jnp.concatenate([st_ref[h].astype(BF16), p["kd"][:, _head(h)]], axis=0))
                   for h in range(HG_HEADS)]

    def outputs(c):
        outs = []
        for h in range(HG_HEADS):
            bh = both[c][h]
            pm = jnp.where(causal, bh[:, HEAD_DIM:], 0.0).astype(BF16)
            outs.append(bh[:, :HEAD_DIM] + _dot(pm, prep[c]["vb"][:, _head(h)]))
        o_s[rows(c), :] = jnp.concatenate(outs, axis=-1)
        del both[c], prep[c]

    tot_min = jnp.zeros((1, HG_WIDTH), F32)
    prepare(order[0])
    for idx, c in enumerate(order):
        if idx + 1 < n_chunks:
            prepare(order[idx + 1])
        carry_and_scores(c)
        if idx > 0:
            outputs(order[idx - 1])
        for h in range(HG_HEADS):
            st_ref[h] = st_ref[h] * prep[c]["dec"][:, _head(h)] + incr[c][h]
        tot_min = jnp.minimum(tot_min, prep[c]["tot"])
        del incr[c]
    outputs(order[-1])
    return tot_min


def _run_recurrence(q_ref, f_ref, v_ref, lbl_ref, direction, st_ref, st_in, o_s):
    reverse = direction == 1
    n_chunks = q_ref.shape[0] // CHUNK
    consts = _decay_consts(lbl_ref, direction)

    st_in[...] = st_ref[...]
    tot_min = _factorised_tile(q_ref, f_ref, v_ref, consts, reverse, st_ref, o_s)

    @pl.when(jnp.min(tot_min) < -SAFE_LOG_DECAY)
    def _():
        st_ref[...] = st_in[...]

        def body(i, carry):
            c = (n_chunks - 1 - i) if reverse else i
            rows = pl.ds(pl.multiple_of(c * CHUNK, CHUNK), CHUNK)
            q, k, b, tot = _chunk_prepare(q_ref[rows, :], f_ref[rows, :], consts, reverse)
            o_s[rows, :] = _exact_chunk(q, k, b, tot, v_ref[rows, :], st_ref, reverse)
            return carry

        lax.fori_loop(0, n_chunks, body, 0)


def _gla_scratch(ts):
    return [pltpu.VMEM((HG_HEADS, HEAD_DIM, HEAD_DIM), F32),
            pltpu.VMEM((HG_HEADS, HEAD_DIM, HEAD_DIM), F32),
            pltpu.VMEM((ts, HG_WIDTH), F32)]


def _gla_bwd_body(q_ref, f_ref, v_ref, lbl_ref, o_ref, st_ref, st_in, o_s):
    @pl.when(pl.program_id(1) == 0)
    def _():
        st_ref[...] = jnp.zeros(st_ref.shape, F32)

    _run_recurrence(q_ref, f_ref, v_ref, lbl_ref, 1, st_ref, st_in, o_s)
    o_ref[...] = o_s[...]


def _gla_bwd(z, lbl, ts):
    bsz, seq, _ = z.shape
    nt = seq // ts

    def col(j):
        return pl.BlockSpec((None, ts, HG_WIDTH), lambda b, i, j=j: (b, nt - 1 - i, j))

    return pl.pallas_call(
        _gla_bwd_body, grid=(bsz, nt),
        in_specs=[col(0), col(2), col(3), _const_spec((2, 2, HG_WIDTH))],
        out_specs=pl.BlockSpec((None, ts, HG_WIDTH), lambda b, i: (b, nt - 1 - i, 0)),
        out_shape=jax.ShapeDtypeStruct((bsz, seq, HG_WIDTH), F32),
        scratch_shapes=_gla_scratch(ts),
        compiler_params=pltpu.CompilerParams(dimension_semantics=("parallel", "arbitrary"),
                                             vmem_limit_bytes=VMEM_LIMIT_BYTES),
        name="gla_bwd",
    )(z, z, z, lbl)


def _meta_state(zm_ref, lbl_ref, st_ref):
    consts = _decay_consts(lbl_ref, 0)
    _, k, b, tot = _chunk_prepare(zm_ref[:, 0:HG_WIDTH], zm_ref[:, HG_WIDTH:2 * HG_WIDTH], consts, False)
    kp = (k * jnp.exp(tot - b)).astype(BF16)
    vb = zm_ref[:, 3 * HG_WIDTH:4 * HG_WIDTH].astype(BF16)
    for h in range(HG_HEADS):
        hs = slice(h * HEAD_DIM, (h + 1) * HEAD_DIM)
        st_ref[h] = _dot_tn(vb[:, hs], kp[:, hs])


def _conv_block(e_s, cw_ref, cb_ref, t0, lane_tile):
    ls = slice(lane_tile * LANES, (lane_tile + 1) * LANES)
    y = jnp.zeros((CONV_ROWS, LANES), F32) + cb_ref[:, ls]
    for r in range(SUBLANES):
        part = None
        for j in range(CONV_KERNEL):
            off = j + HALO - CONV_PAD
            if off % SUBLANES != r:
                continue
            start = t0 + off - r
            term = e_s[start:start + CONV_ROWS + SUBLANES, ls] * cw_ref[j:j + 1, ls]
            part = term if part is None else part + term
        y = y + part[r:r + CONV_ROWS, :]
    return y


def _mix_body(q_ref, f_ref, v_ref, gate_ref, a_ref, g_ref, ap_ref, gp_ref, an_ref, gn_ref, ob_ref, zm_ref,
              lbl_ref, hgn_ref, cw_ref, cb_ref, lng_ref, lnb_ref, o_ref, st_ref, st_in, o_s, e_s):
    i = pl.program_id(1)
    nt = pl.num_programs(1)
    ts = q_ref.shape[0]

    @pl.when(i == 0)
    def _():
        _meta_state(zm_ref, lbl_ref, st_ref)

    _run_recurrence(q_ref, f_ref, v_ref, lbl_ref, 0, st_ref, st_in, o_s)

    gate = gate_ref[...]
    gate = gate * _sigmoid(gate)
    for h in range(HG_HEADS):
        hs = slice(h * HEAD_DIM, (h + 1) * HEAD_DIM)
        o = o_s[:, hs] + ob_ref[:, hs]
        o = o * lax.rsqrt(jnp.mean(o * o, axis=-1, keepdims=True) + RMS_EPS)
        o_ref[:, hs] = o * hgn_ref[:, hs] * gate[:, hs]

    def glu(a, g):
        return a * _sigmoid(g)

    meta_glu = glu(zm_ref[:, 5 * HG_WIDTH:5 * HG_WIDTH + CONV_WIDTH], zm_ref[:, 5 * HG_WIDTH + CONV_WIDTH:])
    prev = jnp.where(i == 0, meta_glu, glu(ap_ref[...], gp_ref[...]))
    nxt = jnp.where(i == nt - 1, 0.0, glu(an_ref[...], gn_ref[...]))
    e_s[0:HALO, :] = prev
    e_s[HALO:HALO + ts, :] = glu(a_ref[...], g_ref[...])
    e_s[HALO + ts:HALO + ts + HALO, :] = nxt
    for t0 in range(0, ts, CONV_ROWS):
        acc = jnp.concatenate([_conv_block(e_s, cw_ref, cb_ref, t0, lt) for lt in range(CONV_WIDTH // LANES)],
                              axis=-1)
        mu = jnp.mean(acc, axis=-1, keepdims=True)
        xc = acc - mu
        y = xc * lax.rsqrt(jnp.mean(xc * xc, axis=-1, keepdims=True) + LN_EPS) * lng_ref[...] + lnb_ref[...]
        o_ref[t0:t0 + CONV_ROWS, HG_WIDTH:] = y * _sigmoid(y)


def _mix(z, ob, zm, w, ts):
    bsz, seq, _ = z.shape
    nt = seq // ts
    hb = ts // HALO
    nhb = seq // HALO

    def col(j):
        return pl.BlockSpec((None, ts, HG_WIDTH), lambda b, i, j=j: (b, i, j))

    def prev_halo(j):
        return pl.BlockSpec((None, HALO, HG_WIDTH), lambda b, i, j=j: (b, jnp.maximum(i * hb - 1, 0), j))

    def next_halo(j):
        return pl.BlockSpec((None, HALO, HG_WIDTH), lambda b, i, j=j: (b, jnp.minimum((i + 1) * hb, nhb - 1), j))

    row = lambda width: _const_spec((1, width))
    return pl.pallas_call(
        _mix_body, grid=(bsz, nt),
        in_specs=[col(0), col(1), col(3), col(4), col(5), col(6),
                  prev_halo(5), prev_halo(6), next_halo(5), next_halo(6),
                  pl.BlockSpec((None, ts, HG_WIDTH), lambda b, i: (b, i, 0)),
                  _const_spec((N_META, Z_WIDTH)), _const_spec((2, 2, HG_WIDTH)),
                  row(HG_WIDTH), _const_spec((CONV_KERNEL, CONV_WIDTH)), row(CONV_WIDTH),
                  row(CONV_WIDTH), row(CONV_WIDTH)],
        out_specs=pl.BlockSpec((None, ts, D_MODEL), lambda b, i: (b, i, 0)),
        out_shape=jax.ShapeDtypeStruct((bsz, seq, D_MODEL), F32),
        scratch_shapes=_gla_scratch(ts) + [pltpu.VMEM((ts + 2 * HALO, CONV_WIDTH), F32)],
        compiler_params=pltpu.CompilerParams(dimension_semantics=("parallel", "arbitrary"),
                                             vmem_limit_bytes=VMEM_LIMIT_BYTES),
        name="mix",
    )(z, z, z, z, z, z, z, z, z, z, ob, zm, w["lbl"], w["hgn"], w["cw"], w["cb"], w["lng"], w["lnb"])


def _prep_weights(ffn1_norm, ffn1_w_gate, ffn1_w_up, ffn1_w_down, mix_norm, w_in, lb_logits, hg_norm, conv_w,
                  conv_b, conv_ln_g, conv_ln_b, w_out, ffn2_norm, ffn2_w_gate, ffn2_w_up, ffn2_w_down, final_norm):
    nc = D_FF // FF_CHUNK

    def cols(wm):
        return wm.astype(BF16).reshape(D_MODEL, nc, FF_CHUNK).transpose(1, 0, 2)

    def rows(wm):
        return wm.astype(BF16).reshape(nc, FF_CHUNK, D_MODEL)

    return dict(
        g1=ffn1_norm[0][None], wg1=cols(ffn1_w_gate[0]), wu1=cols(ffn1_w_up[0]), wd1=rows(ffn1_w_down[0]),
        gm=mix_norm[0][None], win=w_in[0].astype(BF16), lbl=lb_logits.astype(F32), hgn=hg_norm[0][None],
        cw=conv_w[0], cb=conv_b[0][None], lng=conv_ln_g[0][None], lnb=conv_ln_b[0][None],
        wo=w_out[0].astype(BF16),
        g2=ffn2_norm[0][None], wg2=cols(ffn2_w_gate[0]), wu2=cols(ffn2_w_up[0]), wd2=rows(ffn2_w_down[0]),
        gf=final_norm[None])


def _trunk(x, zm, w, tm, ts):
    bsz, seq, _ = x.shape
    xf = x.reshape(bsz * seq, D_MODEL)
    h1 = _ffn1(xf, w, tm)
    z = _inproj(h1, w, tm).reshape(bsz, seq, Z_WIDTH)
    ob = _gla_bwd(z, w["lbl"], ts)
    mix = _mix(z, ob, zm, w, ts).reshape(bsz * seq, D_MODEL)
    return _ffn2(h1, mix, w, tm).reshape(bsz, seq, D_MODEL)


def kernel(x_prompt, x_sample, meta_tokens, ffn1_norm, ffn1_w_gate, ffn1_w_up, ffn1_w_down, mix_norm, w_in,
           lb_logits, hg_norm, conv_w, conv_b, conv_ln_g, conv_ln_b, w_out, ffn2_norm, ffn2_w_gate, ffn2_w_up,
           ffn2_w_down, final_norm):
    tm, ts = TOKEN_TILE, SEQ_TILE
    w = _prep_weights(ffn1_norm, ffn1_w_gate, ffn1_w_up, ffn1_w_down, mix_norm, w_in, lb_logits, hg_norm,
                      conv_w, conv_b, conv_ln_g, conv_ln_b, w_out, ffn2_norm, ffn2_w_gate, ffn2_w_up,
                      ffn2_w_down, final_norm)
    zm = _inproj(_ffn1(meta_tokens.astype(F32), w, N_META), w, N_META)
    return (_trunk(x_prompt, zm, w, tm, ts), _trunk(x_sample, zm, w, tm, ts))
```
